```python
import jax, jax.numpy as jnp
from jax import lax
import numpy as np

D_MODEL = 2048
BATCH = 16
SEQ = 256
DEPTH = 2
DEC_BATCH = 8
DEC_SEQ = 2048
PAST_LEN = 512

GRID_W = 64
HEAD_DIM = 128
N_Q_HEADS = 8
N_KV_HEADS = 2
Q_PER_KV = N_Q_HEADS // N_KV_HEADS
ATT_WIDTH = N_Q_HEADS * HEAD_DIM
KV_WIDTH = N_KV_HEADS * HEAD_DIM
CONV_WIDTH = D_MODEL - ATT_WIDTH
CONF_WIDTH = 31
SHORT_WIDTH = 3
WINDOW = 128
Q_BLOCK = 128
ROPE_THETA = 10000.0
EPS = 1e-6
NEG_INF = -1e30
IN0 = 3 * CONV_WIDTH + 2 * ATT_WIDTH + 2 * KV_WIDTH
IN1 = 4 * CONV_WIDTH + 2 * ATT_WIDTH + 2 * KV_WIDTH

kernel_name = "hybrid_dit_prefix_context_step"


def _split(x, sizes):
    idx, acc = [], 0
    for s in sizes[:-1]:
        acc += s
        idx.append(acc)
    return jnp.split(x, idx, axis=-1)


def rms_norm(x, g):
    xf = x.astype(jnp.float32)
    y = xf * lax.rsqrt(jnp.mean(xf * xf, axis=-1, keepdims=True) + EPS)
    return (y * g.astype(jnp.float32)).astype(x.dtype)


def layer_norm(x, g, b):
    xf = x.astype(jnp.float32)
    mu = jnp.mean(xf, axis=-1, keepdims=True)
    var = jnp.mean(jnp.square(xf - mu), axis=-1, keepdims=True)
    y = (xf - mu) * lax.rsqrt(var + EPS)
    return (y * g.astype(jnp.float32) + b.astype(jnp.float32)).astype(x.dtype)


def adaln(cond, w_mod, b_mod):
    m = jax.nn.silu(cond) @ w_mod + b_mod
    shift, scale, gate = jnp.split(m, 3, axis=-1)
    return shift[:, None, :], scale[:, None, :], gate[:, None, :]


def depthwise_conv(x, w):
    k = w.shape[0]
    return lax.conv_general_dilated(
        x, w[:, None, :].astype(x.dtype), window_strides=(1,),
        padding=[(k // 2, k // 2)], dimension_numbers=("NWC", "WIO", "NWC"),
        feature_group_count=x.shape[-1])


def rope_tables(n_tokens):
    rows = n_tokens // GRID_W
    r = jnp.repeat(jnp.arange(rows), GRID_W).astype(jnp.float32)
    col = jnp.tile(jnp.arange(GRID_W), rows).astype(jnp.float32)
    half = HEAD_DIM // 2
    inv = ROPE_THETA ** (-jnp.arange(0, half, 2, dtype=jnp.float32) / half)
    ang_r = r[:, None] * inv
    ang_c = col[:, None] * inv
    ang = jnp.concatenate([ang_r, ang_r, ang_c, ang_c], axis=-1)
    return jnp.cos(ang)[None, :, None, :], jnp.sin(ang)[None, :, None, :]


def _rot_half(u):
    u1, u2 = jnp.split(u, 2, axis=-1)
    return jnp.concatenate([-u2, u1], axis=-1)


def apply_rope(x, cos, sin):
    xr, xc = jnp.split(x, 2, axis=-1)
    xrot = jnp.concatenate([_rot_half(xr), _rot_half(xc)], axis=-1)
    return (x.astype(jnp.float32) * cos + xrot.astype(jnp.float32) * sin).astype(x.dtype)


def attend(q, k, v, bias, sink):
    s = jnp.einsum("bqhgd,bkhd->bhgqk", q, k).astype(jnp.float32) * (HEAD_DIM ** -0.5)
    if bias is not None:
        s = s + bias
    if sink is None:
        p = jax.nn.softmax(s, axis=-1)
    else:
        sk = sink.astype(jnp.float32).reshape(N_KV_HEADS, Q_PER_KV)[None, :, :, None, None]
        sk = jnp.broadcast_to(sk, s.shape[:-1] + (1,))
        p = jax.nn.softmax(jnp.concatenate([s, sk], axis=-1), axis=-1)[..., :-1]
    return jnp.einsum("bhgqk,bkhd->bqhgd", p.astype(v.dtype), v)


def dense_attention(q, k, v, sink):
    bn, s, h, hd = q.shape
    nb = s // Q_BLOCK
    qb = q.reshape(bn, nb, Q_BLOCK, N_KV_HEADS, Q_PER_KV, hd).swapaxes(0, 1)
    out = lax.map(lambda qi: attend(qi, k, v, None, sink), qb)
    return out.swapaxes(0, 1).reshape(bn, s, h * hd)


def window_attention(q, k, v, k_ctx, v_ctx, sink):
    bn, s, h, hd = q.shape
    nb = s // Q_BLOCK
    span = Q_BLOCK + 2 * WINDOW
    n_ctx = k_ctx.shape[1]
    pad = ((0, 0), (WINDOW, WINDOW), (0, 0), (0, 0))
    kp = jnp.pad(k, pad)
    vp = jnp.pad(v, pad)
    qb = q.reshape(bn, nb, Q_BLOCK, N_KV_HEADS, Q_PER_KV, hd).swapaxes(0, 1)
    ctx_bias = jnp.zeros((Q_BLOCK, n_ctx), jnp.float32)

    def block(args):
        qi, bi = args
        start = bi * Q_BLOCK
        kb = lax.dynamic_slice_in_dim(kp, start, span, axis=1)
        vb = lax.dynamic_slice_in_dim(vp, start, span, axis=1)
        qpos = start + jnp.arange(Q_BLOCK)
        kpos = start - WINDOW + jnp.arange(span)
        valid = ((kpos[None, :] >= 0) & (kpos[None, :] < s)
                 & (jnp.abs(qpos[:, None] - kpos[None, :]) <= WINDOW))
        bias = jnp.concatenate(
            [ctx_bias, jnp.where(valid, 0.0, NEG_INF).astype(jnp.float32)], axis=-1)
        return attend(qi, jnp.concatenate([k_ctx, kb], axis=1),
                      jnp.concatenate([v_ctx, vb], axis=1), bias, sink)

    out = lax.map(block, (qb, jnp.arange(nb)))
    return out.swapaxes(0, 1).reshape(bn, s, h * hd)


def mixer_even(h, w_in, conv_w, conv_b, ln_g, ln_b, qn_g, kn_g, w_out, rope, ctx_kv):
    bn, s, _ = h.shape
    proj = h @ w_in
    ga, gb, gate_a, q, k, v, gate_b = _split(
        proj, [CONV_WIDTH, CONV_WIDTH, CONV_WIDTH, ATT_WIDTH, KV_WIDTH, KV_WIDTH, ATT_WIDTH])
    a = ga * jax.nn.sigmoid(gb)
    a = depthwise_conv(a, conv_w) + conv_b
    a = jax.nn.silu(layer_norm(a, ln_g, ln_b)) * jax.nn.silu(gate_a)
    q = rms_norm(q.reshape(bn, s, N_Q_HEADS, HEAD_DIM), qn_g)
    k = rms_norm(k.reshape(bn, s, N_KV_HEADS, HEAD_DIM), kn_g)
    v = v.reshape(bn, s, N_KV_HEADS, HEAD_DIM)
    if ctx_kv is None:
        k_all, v_all = k, v
    else:
        q = apply_rope(q, *rope)
        k = apply_rope(k, *rope)
        k_all = jnp.concatenate([ctx_kv[0], k], axis=1)
        v_all = jnp.concatenate([ctx_kv[1], v], axis=1)
    b = dense_attention(q, k_all, v_all, None) * jax.nn.silu(gate_b)
    return jnp.concatenate([a, b], axis=-1) @ w_out, k, v


def mixer_odd(h, w_in, sink, short_w, w_out, rope, ctx_kv):
    bn, s, _ = h.shape
    proj = h @ w_in
    q, k, v, gate_c, db, dc, dx, gate_d = _split(
        proj, [ATT_WIDTH, KV_WIDTH, KV_WIDTH, ATT_WIDTH,
               CONV_WIDTH, CONV_WIDTH, CONV_WIDTH, CONV_WIDTH])
    q = q.reshape(bn, s, N_Q_HEADS, HEAD_DIM)
    k = k.reshape(bn, s, N_KV_HEADS, HEAD_DIM)
    v = v.reshape(bn, s, N_KV_HEADS, HEAD_DIM)
    if ctx_kv is None:
        c_out = dense_attention(q, k, v, sink)
    else:
        q = apply_rope(q, *rope)
        kr = apply_rope(k, *rope)
        c_out = window_attention(q, kr, v, ctx_kv[0], ctx_kv[1], sink)
    c_out = c_out * jax.nn.silu(gate_c)
    d_out = db * depthwise_conv(dc * dx, short_w) * jax.nn.silu(gate_d)
    return jnp.concatenate([c_out, d_out], axis=-1) @ w_out, k, v


def setup_inputs(seed: int = 0) -> dict:
    key = jax.random.key(seed)
    ks = iter(jax.random.split(key, 40))
    f32 = jnp.float32

    def nrm(shape, scale):
        return jax.random.normal(next(ks), shape, f32) * scale

    def gain(n):
        return 1.0 + nrm((n,), 0.02)

    d = D_MODEL
    kv_shape = (DEC_BATCH, PAST_LEN, N_KV_HEADS, HEAD_DIM)
    return {
        "x_prompt": nrm((BATCH, SEQ, d), 1.0),
        "x_sample": nrm((DEC_BATCH, DEC_SEQ, d), 1.0),
        "cache_k0": nrm(kv_shape, 1.0),
        "cache_v0": nrm(kv_shape, 1.0),
        "cache_k1": nrm(kv_shape, 1.0),
        "cache_v1": nrm(kv_shape, 1.0),
        "c": nrm((DEC_BATCH, d), 1.0),
        "c_ctx": nrm((d,), 1.0),
        "mod_w0": nrm((d, 3 * d), 0.5 * d ** -0.5),
        "mod_b0": nrm((3 * d,), 0.01),
        "norm_g0": gain(d),
        "w_in0": nrm((d, IN0), d ** -0.5),
        "conv_w0": nrm((CONF_WIDTH, CONV_WIDTH), CONF_WIDTH ** -0.5),
        "conv_b0": nrm((CONV_WIDTH,), 0.01),
        "ln_g0": gain(CONV_WIDTH),
        "ln_b0": nrm((CONV_WIDTH,), 0.01),
        "q_norm_g0": gain(HEAD_DIM),
        "k_norm_g0": gain(HEAD_DIM),
        "w_out0": nrm((d, d), d ** -0.5),
        "mod_w1": nrm((d, 3 * d), 0.5 * d ** -0.5),
        "mod_b1": nrm((3 * d,), 0.01),
        "norm_g1": gain(d),
        "w_in1": nrm((d, IN1), d ** -0.5),
        "sink1": nrm((N_Q_HEADS,), 0.5),
        "short_w1": nrm((SHORT_WIDTH, CONV_WIDTH), SHORT_WIDTH ** -0.5),
        "w_out1": nrm((d, d), d ** -0.5),
        "final_norm_g": gain(d),
    }


def reference(x_prompt, x_sample, cache_k0, cache_v0, cache_k1, cache_v1, c, c_ctx,
              mod_w0, mod_b0, norm_g0, w_in0, conv_w0, conv_b0, ln_g0, ln_b0,
              q_norm_g0, k_norm_g0, w_out0,
              mod_w1, mod_b1, norm_g1, w_in1, sink1, short_w1, w_out1, final_norm_g):
    rope = rope_tables(x_sample.shape[1])
    caches = [(cache_k0, cache_v0), (cache_k1, cache_v1)]
    mods = [(mod_w0, mod_b0, norm_g0), (mod_w1, mod_b1, norm_g1)]
    cond_ctx = c_ctx[None, :]
    xp, xs = x_prompt, x_sample
    new_kv = []
    for layer in range(DEPTH):
        w_mod, b_mod, g = mods[layer]
        sh, sc, gt = adaln(cond_ctx, w_mod, b_mod)
        hp = rms_norm(xp, g) * (1.0 + sc) + sh
        sh2, sc2, gt2 = adaln(c, w_mod, b_mod)
        hs = rms_norm(xs, g) * (1.0 + sc2) + sh2
        if layer % 2 == 0:
            op, kc, vc = mixer_even(hp, w_in0, conv_w0, conv_b0, ln_g0, ln_b0,
                                    q_norm_g0, k_norm_g0, w_out0, rope, None)
            os_, _, _ = mixer_even(hs, w_in0, conv_w0, conv_b0, ln_g0, ln_b0,
                                   q_norm_g0, k_norm_g0, w_out0, rope, caches[layer])
        else:
            op, kc, vc = mixer_odd(hp, w_in1, sink1, short_w1, w_out1, rope, None)
            os_, _, _ = mixer_odd(hs, w_in1, sink1, short_w1, w_out1, rope, caches[layer])
        new_kv.append((kc, vc))
        xp = xp + gt * op
        xs = xs + gt2 * os_
    y_prompt = rms_norm(xp, final_norm_g)
    y_sample = rms_norm(xs, final_norm_g)
    return (y_prompt, y_sample, new_kv[0][0], new_kv[0][1], new_kv[1][0], new_kv[1][1])
```

```python
import functools

import jax
import jax.numpy as jnp
from jax import lax
from jax.experimental import pallas as pl
from jax.experimental.pallas import tpu as pltpu

F32 = jnp.float32
BF16 = jnp.bfloat16

D_MODEL = 2048
GRID_W = 64
HEAD_DIM = 128
N_Q_HEADS = 8
N_KV_HEADS = 2
Q_PER_KV = N_Q_HEADS // N_KV_HEADS
ATT_WIDTH = N_Q_HEADS * HEAD_DIM
KV_WIDTH = N_KV_HEADS * HEAD_DIM
CONV_WIDTH = D_MODEL - ATT_WIDTH
CONF_WIDTH = 31
SHORT_WIDTH = 3
WINDOW = 128
ROPE_THETA = 10000.0
EPS = 1e-6
NEG_INF = -1e30
QGROUP = Q_PER_KV * HEAD_DIM

L0_GA, L0_GB, L0_GATE_A = 0, CONV_WIDTH, 2 * CONV_WIDTH
L0_Q = 3 * CONV_WIDTH
L0_K = L0_Q + ATT_WIDTH
L0_V = L0_K + KV_WIDTH
L0_GATE_B = L0_V + KV_WIDTH
L1_Q = 0
L1_K = ATT_WIDTH
L1_V = L1_K + KV_WIDTH
L1_GATE_C = L1_V + KV_WIDTH
L1_DB = L1_GATE_C + ATT_WIDTH
L1_DC = L1_DB + CONV_WIDTH
L1_DX = L1_DC + CONV_WIDTH
L1_GATE_D = L1_DX + CONV_WIDTH

LANES = 128
SUBLANES = 8
VMEM_LIMIT = 48 * 1024 * 1024

MOD_ROWS = 16
TM_PROJ = 512
TN_PROJ = 512
TN_MOD = 512
TS_CONV = 256
CONV_HALO = 16
SHORT_HALO = 8
CONV_ROWS = 64
TQ = 128
TK = 512
TR_KPREP = 512


def _silu(x):
    return x * jax.nn.sigmoid(x)


def _params(sem):
    return pltpu.CompilerParams(dimension_semantics=sem, vmem_limit_bytes=VMEM_LIMIT)


def _mod_kernel(c_ref, w_ref, b_ref, o_ref):
    s = _silu(c_ref[...]).astype(BF16)
    o_ref[...] = jnp.dot(s, w_ref[...].astype(BF16), preferred_element_type=F32) + b_ref[...]


def _modulation(cond, w_mod, b_mod):
    d = cond.shape[1]
    n = w_mod.shape[1]
    return pl.pallas_call(
        _mod_kernel,
        grid=(n // TN_MOD,),
        in_specs=[pl.BlockSpec((MOD_ROWS, d), lambda j: (0, 0)),
                  pl.BlockSpec((d, TN_MOD), lambda j: (0, j)),
                  pl.BlockSpec((1, TN_MOD), lambda j: (0, j))],
        out_specs=pl.BlockSpec((MOD_ROWS, TN_MOD), lambda j: (0, j)),
        out_shape=jax.ShapeDtypeStruct((MOD_ROWS, n), F32),
        compiler_params=_params(("arbitrary",)),
        name="modulation",
    )(cond, w_mod, b_mod.reshape(1, n))


def _inproj_kernel(x_ref, m_ref, g_ref, w_ref, o_ref, h_ref):
    @pl.when(pl.program_id(1) == 0)
    def _():
        x = x_ref[...]
        ms = jnp.mean(x * x, axis=-1, keepdims=True)
        y = x * lax.rsqrt(ms + EPS) * g_ref[...]
        shift = m_ref[0, :, 0:D_MODEL]
        scale = m_ref[0, :, D_MODEL:2 * D_MODEL]
        h_ref[...] = (y * (1.0 + scale) + shift).astype(BF16)

    o_ref[...] = jnp.dot(h_ref[...], w_ref[...], preferred_element_type=F32)


def _inproj(x2d, mod3d, mod_row, g, w_bf16):
    t, d = x2d.shape
    n = w_bf16.shape[1]
    return pl.pallas_call(
        _inproj_kernel,
        grid=(t // TM_PROJ, n // TN_PROJ),
        in_specs=[pl.BlockSpec((TM_PROJ, d), lambda i, j: (i, 0)),
                  pl.BlockSpec((1, 1, 3 * d), lambda i, j: (mod_row(i * TM_PROJ), 0, 0)),
                  pl.BlockSpec((1, d), lambda i, j: (0, 0)),
                  pl.BlockSpec((d, TN_PROJ), lambda i, j: (0, j))],
        out_specs=pl.BlockSpec((TM_PROJ, TN_PROJ), lambda i, j: (i, j)),
        out_shape=jax.ShapeDtypeStruct((t, n), F32),
        scratch_shapes=[pltpu.VMEM((TM_PROJ, d), BF16)],
        compiler_params=_params(("parallel", "arbitrary")),
        name="inproj",
    )(x2d, mod3d, g.reshape(1, d), w_bf16)


def _conv0_kernel(ga_ref, gb_ref, gt_ref, gap_ref, gbp_ref, gan_ref, gbn_ref,
                  w_ref, cb_ref, lg_ref, lb_ref, o_ref, buf_ref, conv_ref):
    i = pl.program_id(1)
    last = pl.num_programs(1) - 1
    ts = ga_ref.shape[0]
    c = ga_ref.shape[1]
    prev = gap_ref[...] * jax.nn.sigmoid(gbp_ref[...])
    nxt = gan_ref[...] * jax.nn.sigmoid(gbn_ref[...])
    buf_ref[0:CONV_HALO, :] = jnp.where(i > 0, prev, 0.0)
    buf_ref[CONV_HALO:CONV_HALO + ts, :] = ga_ref[...] * jax.nn.sigmoid(gb_ref[...])
    buf_ref[CONV_HALO + ts:2 * CONV_HALO + ts, :] = jnp.where(i < last, nxt, 0.0)

    base = CONV_HALO - CONF_WIDTH // 2
    for r in range(ts // CONV_ROWS):
        for cc in range(c // LANES):
            lanes = slice(cc * LANES, (cc + 1) * LANES)
            acc = jnp.broadcast_to(cb_ref[:, lanes], (CONV_ROWS, LANES))
            for k in range(CONF_WIDTH):
                r0 = r * CONV_ROWS + base + k
                acc = acc + buf_ref[r0:r0 + CONV_ROWS, lanes] * w_ref[k:k + 1, lanes]
            conv_ref[r * CONV_ROWS:(r + 1) * CONV_ROWS, lanes] = acc

    y = conv_ref[...]
    mu = jnp.mean(y, axis=-1, keepdims=True)
    yc = y - mu
    var = jnp.mean(yc * yc, axis=-1, keepdims=True)
    ln = yc * lax.rsqrt(var + EPS) * lg_ref[...] + lb_ref[...]
    o_ref[...] = (_silu(ln) * _silu(gt_ref[...])).astype(BF16)


def _conv0(proj, batch, seq, conv_w, conv_b, ln_g, ln_b):
    c = CONV_WIDTH
    nt = seq // TS_CONV
    hb = TS_CONV // CONV_HALO
    nh = seq // CONV_HALO

    def cur(col):
        return pl.BlockSpec((TS_CONV, c), lambda b, i: (b * nt + i, col))

    def prev(col):
        return pl.BlockSpec((CONV_HALO, c), lambda b, i: (b * nh + jnp.maximum(i * hb - 1, 0), col))

    def nxt(col):
        return pl.BlockSpec((CONV_HALO, c), lambda b, i: (b * nh + jnp.minimum((i + 1) * hb, nh - 1), col))

    def const(rows):
        return pl.BlockSpec((rows, c), lambda b, i: (0, 0))

    ca, cb, cg = L0_GA // c, L0_GB // c, L0_GATE_A // c
    return pl.pallas_call(
        _conv0_kernel,
        grid=(batch, nt),
        in_specs=[cur(ca), cur(cb), cur(cg), prev(ca), prev(cb), nxt(ca), nxt(cb),
                  const(CONF_WIDTH), const(1), const(1), const(1)],
        out_specs=pl.BlockSpec((TS_CONV, c), lambda b, i: (b * nt + i, 0)),
        out_shape=jax.ShapeDtypeStruct((batch * seq, c), BF16),
        scratch_shapes=[pltpu.VMEM((TS_CONV + 2 * CONV_HALO, c), F32),
                        pltpu.VMEM((TS_CONV, c), F32)],
        compiler_params=_params(("parallel", "arbitrary")),
        name="conv0",
    )(proj, proj, proj, proj, proj, proj, proj,
      conv_w, conv_b.reshape(1, c), ln_g.reshape(1, c), ln_b.reshape(1, c))


def _conv1_kernel(db_ref, dc_ref, dx_ref, gd_ref, dcp_ref, dxp_ref, dcn_ref, dxn_ref,
                  w_ref, o_ref, buf_ref):
    i = pl.program_id(1)
    last = pl.num_programs(1) - 1
    ts = db_ref.shape[0]
    buf_ref[0:SHORT_HALO, :] = jnp.where(i > 0, dcp_ref[...] * dxp_ref[...], 0.0)
    buf_ref[SHORT_HALO:SHORT_HALO + ts, :] = dc_ref[...] * dx_ref[...]
    buf_ref[SHORT_HALO + ts:2 * SHORT_HALO + ts, :] = jnp.where(i < last, dcn_ref[...] * dxn_ref[...], 0.0)
    base = SHORT_HALO - SHORT_WIDTH // 2
    conv = buf_ref[base:base + ts, :] * w_ref[0:1, :]
    for k in range(1, SHORT_WIDTH):
        conv = conv + buf_ref[base + k:base + k + ts, :] * w_ref[k:k + 1, :]
    o_ref[...] = (db_ref[...] * conv * _silu(gd_ref[...])).astype(BF16)


def _conv1(proj, batch, seq, short_w):
    cw = CONV_WIDTH // 2
    nt = seq // TS_CONV
    hb = TS_CONV // SHORT_HALO
    nh = seq // SHORT_HALO

    def cur(off):
        return pl.BlockSpec((TS_CONV, cw), lambda b, i, cb: (b * nt + i, off // cw + cb))

    def prev(off):
        return pl.BlockSpec((SHORT_HALO, cw),
                            lambda b, i, cb: (b * nh + jnp.maximum(i * hb - 1, 0), off // cw + cb))

    def nxt(off):
        return pl.BlockSpec((SHORT_HALO, cw),
                            lambda b, i, cb: (b * nh + jnp.minimum((i + 1) * hb, nh - 1), off // cw + cb))

    return pl.pallas_call(
        _conv1_kernel,
        grid=(batch, nt, CONV_WIDTH // cw),
        in_specs=[cur(L1_DB), cur(L1_DC), cur(L1_DX), cur(L1_GATE_D),
                  prev(L1_DC), prev(L1_DX), nxt(L1_DC), nxt(L1_DX),
                  pl.BlockSpec((SHORT_WIDTH, cw), lambda b, i, cb: (0, cb))],
        out_specs=pl.BlockSpec((TS_CONV, cw), lambda b, i, cb: (b * nt + i, cb)),
        out_shape=jax.ShapeDtypeStruct((batch * seq, CONV_WIDTH), BF16),
        scratch_shapes=[pltpu.VMEM((TS_CONV + 2 * SHORT_HALO, cw), F32)],
        compiler_params=_params(("parallel", "arbitrary", "arbitrary")),
        name="conv1",
    )(proj, proj, proj, proj, proj, proj, proj, proj, short_w)


def _rope_tables(n_tokens):
    rows = n_tokens // GRID_W
    r = jnp.repeat(jnp.arange(rows), GRID_W).astype(F32)
    col = jnp.tile(jnp.arange(GRID_W), rows).astype(F32)
    half = HEAD_DIM // 2
    inv = ROPE_THETA ** (-jnp.arange(0, half, 2, dtype=F32) / half)
    ang_r = r[:, None] * inv
    ang_c = col[:, None] * inv
    ang = jnp.concatenate([ang_r, ang_r, ang_c, ang_c], axis=-1)
    cos, sin = jnp.cos(ang), jnp.sin(ang)
    lane = jnp.arange(HEAD_DIM)
    first = (lane % half) < (half // 2)
    sin_lo = jnp.where(first, -sin, 0.0)
    sin_hi = jnp.where(first, 0.0, sin)
    return cos, sin_lo, sin_hi


def _rope(x, cos, sin_lo, sin_hi):
    q = HEAD_DIM // 4
    return (x * cos + pltpu.roll(x, HEAD_DIM - q, 1) * sin_lo + pltpu.roll(x, q, 1) * sin_hi)


def _head_rms(x, g):
    return x * lax.rsqrt(jnp.mean(x * x, axis=-1, keepdims=True) + EPS) * g


def _kprep_kernel(*refs, norm, rope):
    refs = list(refs)
    k_ref = refs.pop(0)
    g_ref = refs.pop(0) if norm else None
    tabs = [refs.pop(0)[...] for _ in range(3)] if rope else None
    o_ref = refs.pop(0)
    for h in range(N_KV_HEADS):
        lanes = slice(h * HEAD_DIM, (h + 1) * HEAD_DIM)
        k = k_ref[:, lanes]
        if norm:
            k = _head_rms(k, g_ref[...])
        if rope:
            k = _rope(k, *tabs)
        o_ref[:, lanes] = k


def _kprep(proj, k_off, seq, norm_g, tables):
    t = proj.shape[0]
    nt = seq // TR_KPREP if tables is not None else 1
    args = [proj]
    in_specs = [pl.BlockSpec((TR_KPREP, KV_WIDTH), lambda i: (i, k_off // KV_WIDTH))]
    if norm_g is not None:
        args.append(norm_g.reshape(1, HEAD_DIM))
        in_specs.append(pl.BlockSpec((1, HEAD_DIM), lambda i: (0, 0)))
    if tables is not None:
        args += list(tables)
        in_specs += [pl.BlockSpec((TR_KPREP, HEAD_DIM), lambda i: (i % nt, 0))] * 3
    return pl.pallas_call(
        functools.partial(_kprep_kernel, norm=norm_g is not None, rope=tables is not None),
        grid=(t // TR_KPREP,),
        in_specs=in_specs,
        out_specs=pl.BlockSpec((TR_KPREP, KV_WIDTH), lambda i: (i, 0)),
        out_shape=jax.ShapeDtypeStruct((t, KV_WIDTH), F32),
        compiler_params=_params(("parallel",)),
        name="kprep",
    )(*args)


def _attn_kernel(*refs, qnorm, rope, ctx, window, sink, seq):
    refs = list(refs)
    q_ref = refs.pop(0)
    qg_ref = refs.pop(0) if qnorm else None
    tabs = [refs.pop(0)[...] for _ in range(3)] if rope else None
    kc_ref, vc_ref = (refs.pop(0), refs.pop(0)) if ctx else (None, None)
    k_ref, v_ref, gate_ref = refs.pop(0), refs.pop(0), refs.pop(0)
    sink_ref = refs.pop(0) if sink else None
    o_ref = refs.pop(0)

    h = pl.program_id(1)
    qi = pl.program_id(2)
    tq = q_ref.shape[0]
    rows = Q_PER_KV * tq

    qs = []
    for g in range(Q_PER_KV):
        q = q_ref[:, g * HEAD_DIM:(g + 1) * HEAD_DIM]
        if qnorm:
            q = _head_rms(q, qg_ref[...])
        if rope:
            q = _rope(q, *tabs)
        qs.append(q * (HEAD_DIM ** -0.5))
    qst = jnp.concatenate(qs, axis=0).astype(BF16)

    if sink:
        row_head = lax.broadcasted_iota(jnp.int32, (rows, 1), 0) // tq
        m = jnp.zeros((rows, 1), F32)
        for g in range(Q_PER_KV):
            m = jnp.where(row_head == g, sink_ref[h * Q_PER_KV + g], m)
        l = jnp.ones((rows, 1), F32)
    else:
        m = jnp.full((rows, 1), NEG_INF, F32)
        l = jnp.zeros((rows, 1), F32)
    acc = jnp.zeros((rows, HEAD_DIM), F32)

    def step(carry, k, v, bias):
        m, l, acc = carry
        s = lax.dot_general(qst, k.astype(BF16), (((1,), (1,)), ((), ())),
                            preferred_element_type=F32)
        if bias is not None:
            s = s + bias
        m_new = jnp.maximum(m, jnp.max(s, axis=-1, keepdims=True))
        alpha = jnp.exp(m - m_new)
        p = jnp.exp(s - m_new)
        l = alpha * l + jnp.sum(p, axis=-1, keepdims=True)
        acc = alpha * acc + jnp.dot(p.astype(BF16), v.astype(BF16), preferred_element_type=F32)
        return m_new, l, acc

    carry = (m, l, acc)
    if ctx:
        n_ctx = kc_ref.shape[0]
        for c0 in range(0, n_ctx, TK):
            c1 = min(c0 + TK, n_ctx)
            carry = step(carry, kc_ref[c0:c1, :], vc_ref[c0:c1, :], None)
    if window:
        span = tq + 2 * WINDOW
        start = pl.multiple_of(jnp.clip(qi * tq - WINDOW, 0, seq - span), LANES)
        qpos = qi * tq + lax.broadcasted_iota(jnp.int32, (tq, span), 0)
        kpos = start + lax.broadcasted_iota(jnp.int32, (tq, span), 1)
        bias = jnp.where(jnp.abs(qpos - kpos) <= WINDOW, 0.0, NEG_INF).astype(F32)
        bias = jnp.concatenate([bias] * Q_PER_KV, axis=0)
        carry = step(carry, k_ref[pl.ds(start, span), :], v_ref[pl.ds(start, span), :], bias)
    else:
        for c0 in range(0, seq, TK):
            c1 = min(c0 + TK, seq)
            carry = step(carry, k_ref[c0:c1, :], v_ref[c0:c1, :], None)

    m, l, acc = carry
    out = acc * (1.0 / l)
    out = jnp.concatenate([out[g * tq:(g + 1) * tq, :] for g in range(Q_PER_KV)], axis=1)
    o_ref[...] = (out * _silu(gate_ref[...])).astype(BF16)


def _attention(proj, kprep, batch, seq, q_off, v_off, gate_off, *, k_off=None,
               q_norm_g=None, tables=None, ctx_kv=None, window=False, sink=None):
    nq = seq // TQ
    args = [proj]
    in_specs = [pl.BlockSpec((TQ, QGROUP), lambda b, h, i: (b * nq + i, q_off // QGROUP + h))]
    if q_norm_g is not None:
        args.append(q_norm_g.reshape(1, HEAD_DIM))
        in_specs.append(pl.BlockSpec((1, HEAD_DIM), lambda b, h, i: (0, 0)))
    if tables is not None:
        args += list(tables)
        in_specs += [pl.BlockSpec((TQ, HEAD_DIM), lambda b, h, i: (i, 0))] * 3
    if ctx_kv is not None:
        n_ctx = ctx_kv[0].shape[1]
        for cache in ctx_kv:
            args.append(cache.reshape(batch * n_ctx, KV_WIDTH))
            in_specs.append(pl.BlockSpec((n_ctx, HEAD_DIM), lambda b, h, i: (b, h)))
    if kprep is not None:
        args.append(kprep)
        in_specs.append(pl.BlockSpec((seq, HEAD_DIM), lambda b, h, i: (b, h)))
    else:
        args.append(proj)
        in_specs.append(pl.BlockSpec((seq, HEAD_DIM), lambda b, h, i: (b, k_off // HEAD_DIM + h)))
    args.append(proj)
    in_specs.append(pl.BlockSpec((seq, HEAD_DIM), lambda b, h, i: (b, v_off // HEAD_DIM + h)))
    args.append(proj)
    in_specs.append(pl.BlockSpec((TQ, QGROUP), lambda b, h, i: (b * nq + i, gate_off // QGROUP + h)))
    if sink is not None:
        args.append(sink)
        in_specs.append(pl.BlockSpec(memory_space=pltpu.SMEM))
    return pl.pallas_call(
        functools.partial(_attn_kernel, qnorm=q_norm_g is not None, rope=tables is not None,
                          ctx=ctx_kv is not None, window=window, sink=sink is not None, seq=seq),
        grid=(batch, N_KV_HEADS, nq),
        in_specs=in_specs,
        out_specs=pl.BlockSpec((TQ, QGROUP), lambda b, h, i: (b * nq + i, h)),
        out_shape=jax.ShapeDtypeStruct((batch * seq, ATT_WIDTH), BF16),
        compiler_params=_params(("parallel", "parallel", "arbitrary")),
        name="attention",
    )(*args)


def _outproj_kernel(*refs, final):
    refs = list(refs)
    a_ref, b_ref, x_ref, m_ref, w_ref = (refs.pop(0) for _ in range(5))
    fg_ref = refs.pop(0) if final else None
    o_ref = refs.pop(0)
    ka = a_ref.shape[1]
    y = jnp.dot(a_ref[...], w_ref[0:ka, :], preferred_element_type=F32)
    y = y + jnp.dot(b_ref[...], w_ref[ka:, :], preferred_element_type=F32)
    gate = m_ref[0, :, 2 * D_MODEL:3 * D_MODEL]
    xn = x_ref[...] + gate * y
    if final:
        xn = xn * lax.rsqrt(jnp.mean(xn * xn, axis=-1, keepdims=True) + EPS) * fg_ref[...]
    o_ref[...] = xn


def _outproj(first, second, x2d, mod3d, mod_row, w_bf16, final_g=None):
    t, d = x2d.shape
    ka, kb = first.shape[1], second.shape[1]
    tm = TM_PROJ // 2
    args = [first, second, x2d, mod3d, w_bf16]
    in_specs = [pl.BlockSpec((tm, ka), lambda i: (i, 0)),
                pl.BlockSpec((tm, kb), lambda i: (i, 0)),
                pl.BlockSpec((tm, d), lambda i: (i, 0)),
                pl.BlockSpec((1, 1, 3 * d), lambda i: (mod_row(i * tm), 0, 0)),
                pl.BlockSpec((ka + kb, d), lambda i: (0, 0))]
    if final_g is not None:
        args.append(final_g.reshape(1, d))
        in_specs.append(pl.BlockSpec((1, d), lambda i: (0, 0)))
    return pl.pallas_call(
        functools.partial(_outproj_kernel, final=final_g is not None),
        grid=(t // tm,),
        in_specs=in_specs,
        out_specs=pl.BlockSpec((tm, d), lambda i: (i, 0)),
        out_shape=jax.ShapeDtypeStruct((t, d), F32),
        compiler_params=_params(("parallel",)),
        name="outproj",
    )(*args)


def kernel(x_prompt, x_sample, cache_k0, cache_v0, cache_k1, cache_v1, c, c_ctx,
           mod_w0, mod_b0, norm_g0, w_in0, conv_w0, conv_b0, ln_g0, ln_b0,
           q_norm_g0, k_norm_g0, w_out0,
           mod_w1, mod_b1, norm_g1, w_in1, sink1, short_w1, w_out1, final_norm_g):
    bp, sp, d = x_prompt.shape
    bs, ss, _ = x_sample.shape
    tables = _rope_tables(ss)

    ctx_row = bs
    cond = jnp.zeros((MOD_ROWS, d), F32).at[:bs].set(c).at[ctx_row].set(c_ctx)
    mod0 = _modulation(cond, mod_w0, mod_b0).reshape(MOD_ROWS, 1, 3 * d)
    mod1 = _modulation(cond, mod_w1, mod_b1).reshape(MOD_ROWS, 1, 3 * d)

    def prompt_row(_):
        return ctx_row

    def sample_row(row):
        return row // ss

    w_in0b, w_out0b = w_in0.astype(BF16), w_out0.astype(BF16)
    w_in1b, w_out1b = w_in1.astype(BF16), w_out1.astype(BF16)

    xp = x_prompt.reshape(bp * sp, d)
    xs = x_sample.reshape(bs * ss, d)

    pp = _inproj(xp, mod0, prompt_row, norm_g0, w_in0b)
    ps = _inproj(xs, mod0, sample_row, norm_g0, w_in0b)
    ap = _conv0(pp, bp, sp, conv_w0, conv_b0, ln_g0, ln_b0)
    as_ = _conv0(ps, bs, ss, conv_w0, conv_b0, ln_g0, ln_b0)
    kp0 = _kprep(pp, L0_K, sp, k_norm_g0, None)
    ks0 = _kprep(ps, L0_K, ss, k_norm_g0, tables)
    bp_ = _attention(pp, kp0, bp, sp, L0_Q, L0_V, L0_GATE_B, q_norm_g=q_norm_g0)
    bs_ = _attention(ps, ks0, bs, ss, L0_Q, L0_V, L0_GATE_B, q_norm_g=q_norm_g0,
                     tables=tables, ctx_kv=(cache_k0, cache_v0))
    xp = _outproj(ap, bp_, xp, mod0, prompt_row, w_out0b)
    xs = _outproj(as_, bs_, xs, mod0, sample_row, w_out0b)
    new_k0 = kp0.reshape(bp, sp, N_KV_HEADS, HEAD_DIM)
    new_v0 = pp[:, L0_V:L0_V + KV_WIDTH].reshape(bp, sp, N_KV_HEADS, HEAD_DIM)

    pp = _inproj(xp, mod1, prompt_row, norm_g1, w_in1b)
    ps = _inproj(xs, mod1, sample_row, norm_g1, w_in1b)
    ks1 = _kprep(ps, L1_K, ss, None, tables)
    cp = _attention(pp, None, bp, sp, L1_Q, L1_V, L1_GATE_C, k_off=L1_K, sink=sink1)
    cs = _attention(ps, ks1, bs, ss, L1_Q, L1_V, L1_GATE_C, tables=tables,
                    ctx_kv=(cache_k1, cache_v1), window=True, sink=sink1)
    dp = _conv1(pp, bp, sp, short_w1)
    ds = _conv1(ps, bs, ss, short_w1)
    yp = _outproj(cp, dp, xp, mod1, prompt_row, w_out1b, final_norm_g)
    ys = _outproj(cs, ds, xs, mod1, sample_row, w_out1b, final_norm_g)
    new_k1 = pp[:, L1_K:L1_K + KV_WIDTH].reshape(bp, sp, N_KV_HEADS, HEAD_DIM)
    new_v1 = pp[:, L1_V:L1_V + KV_WIDTH].reshape(bp, sp, N_KV_HEADS, HEAD_DIM)

    return (yp.reshape(bp, sp, d), ys.reshape(bs, ss, d), new_k0, new_v0, new_k1, new_v1)
```
